```python
import math
import jax, jax.numpy as jnp
from jax import lax
import numpy as np

D_MODEL = 1024
BATCH = 8
SEQ = 2048
DEPTH = 2

CHUNK = 64
D_CONV = 1024
CONV_GROUPS = 16
CONV_WIDTH = 3
D_SSM = 512
SSM_GROUP = 16
N_SSM_GROUPS = D_SSM // SSM_GROUP
SSM_STATE = 64
N_BRANCH = 2
PROJ_WIDTH = 3 * D_CONV + D_SSM + N_BRANCH * D_MODEL
D_FF = 7 * D_MODEL // 2
N_EXPERTS = 8
TOP_K = 2
N_DENSE = (DEPTH + 1) // 2
N_MOE = DEPTH // 2
RMS_EPS = 1e-6
DT_MIN = 1e-3
DT_MAX = 1e-1

kernel_name = "hybrid_conv_s5_moe_encoder"


def rmsnorm(x, g):
    xf = x.astype(jnp.float32)
    y = xf * lax.rsqrt(jnp.mean(xf * xf, axis=-1, keepdims=True) + RMS_EPS)
    return (y * g.astype(jnp.float32)).astype(x.dtype)


def short_conv_branch(b_gate, c_gate, v, conv_w, w_out):
    u = c_gate * v
    rhs = conv_w.reshape(CONV_WIDTH, 1, D_CONV).astype(u.dtype)
    conv = lax.conv_general_dilated(
        u, rhs, window_strides=(1,), padding=[(CONV_WIDTH - 1, 0)],
        dimension_numbers=("NWC", "WIO", "NWC"), feature_group_count=D_CONV)
    return (b_gate * conv) @ w_out


def _ssm_combine(e_i, e_j):
    ar_i, ai_i, br_i, bi_i = e_i
    ar_j, ai_j, br_j, bi_j = e_j
    return (ar_j * ar_i - ai_j * ai_i,
            ar_j * ai_i + ai_j * ar_i,
            ar_j * br_i - ai_j * bi_i + br_j,
            ar_j * bi_i + ai_j * br_i + bi_j)


def s5_branch(u, a_re, a_im, log_dt, b_re, b_im, c_re, c_im, d_skip, w_glu):
    bsz, seq, _ = u.shape
    f32 = jnp.float32
    uf = u.astype(f32).reshape(bsz, seq, N_SSM_GROUPS, SSM_GROUP)
    ar = a_re.astype(f32)
    ai = a_im.astype(f32)
    dt = jnp.exp(log_dt.astype(f32))[:, None]
    mag = jnp.exp(ar * dt)
    abar_re = mag * jnp.cos(ai * dt)
    abar_im = mag * jnp.sin(ai * dt)
    den = ar * ar + ai * ai
    nr = abar_re - 1.0
    ni = abar_im
    coef_re = (nr * ar + ni * ai) / den
    coef_im = (ni * ar - nr * ai) / den
    br = b_re.astype(f32)
    bi = b_im.astype(f32)
    bb_re = coef_re[..., None] * br - coef_im[..., None] * bi
    bb_im = coef_re[..., None] * bi + coef_im[..., None] * br
    bu_re = jnp.einsum("blgc,gnc->lbgn", uf, bb_re)
    bu_im = jnp.einsum("blgc,gnc->lbgn", uf, bb_im)
    a_seq_re = jnp.broadcast_to(abar_re, (seq, 1, N_SSM_GROUPS, SSM_STATE))
    a_seq_im = jnp.broadcast_to(abar_im, (seq, 1, N_SSM_GROUPS, SSM_STATE))
    _, _, s_re, s_im = lax.associative_scan(
        _ssm_combine, (a_seq_re, a_seq_im, bu_re, bu_im), axis=0)
    y = (jnp.einsum("gcn,lbgn->blgc", c_re.astype(f32), s_re)
         - jnp.einsum("gcn,lbgn->blgc", c_im.astype(f32), s_im))
    y = y.reshape(bsz, seq, D_SSM) + d_skip.astype(f32) * u.astype(f32)
    y = jax.nn.gelu(y).astype(u.dtype)
    z = y @ w_glu
    return z[..., :D_MODEL] * jax.nn.sigmoid(z[..., D_MODEL:])


def swiglu(h, w_gate, w_up, w_down):
    return (jax.nn.silu(h @ w_gate) * (h @ w_up)) @ w_down


def moe_swiglu(h, router_w, router_b, w_gate, w_up, w_down):
    logits = (h @ router_w).astype(jnp.float32) + router_b.astype(jnp.float32)
    top_val, top_idx = lax.top_k(logits, TOP_K)
    top_w = jax.nn.softmax(top_val, axis=-1)
    gate = jnp.sum(jax.nn.one_hot(top_idx, N_EXPERTS, dtype=jnp.float32) * top_w[..., None],
                   axis=-2).astype(h.dtype)
    out = jnp.zeros_like(h)
    for e in range(N_EXPERTS):
        out = out + gate[..., e:e + 1] * swiglu(h, w_gate[e], w_up[e], w_down[e])
    return out


def setup_inputs(seed: int = 0) -> dict:
    key = jax.random.key(seed)
    ks = jax.random.split(key, 32)
    f32 = jnp.float32
    nrm = lambda k, shape, scale: jax.random.normal(k, shape, f32) * scale
    n_idx = jnp.arange(SSM_STATE, dtype=f32)
    ssm_a_re = -0.5 + nrm(ks[5], (DEPTH, N_SSM_GROUPS, SSM_STATE), 0.01)
    ssm_a_im = math.pi * n_idx + nrm(ks[6], (DEPTH, N_SSM_GROUPS, SSM_STATE), 0.01)
    ssm_log_dt = jax.random.uniform(ks[7], (DEPTH, N_SSM_GROUPS), f32,
                                    math.log(DT_MIN), math.log(DT_MAX))
    return {
        "x": nrm(ks[0], (BATCH, SEQ, D_MODEL), 1.0),
        "norm_mix": 1.0 + nrm(ks[1], (DEPTH, D_MODEL), 0.02),
        "w_in": nrm(ks[2], (DEPTH, D_MODEL, PROJ_WIDTH), D_MODEL ** -0.5),
        "b_in": nrm(ks[3], (DEPTH, PROJ_WIDTH), 0.02),
        "conv_w": nrm(ks[4], (DEPTH, CONV_WIDTH, D_CONV), CONV_WIDTH ** -0.5),
        "w_conv_out": nrm(ks[8], (DEPTH, D_CONV, D_MODEL), D_CONV ** -0.5),
        "ssm_a_re": ssm_a_re,
        "ssm_a_im": ssm_a_im,
        "ssm_log_dt": ssm_log_dt,
        "ssm_b_re": nrm(ks[9], (DEPTH, N_SSM_GROUPS, SSM_STATE, SSM_GROUP), (2 * SSM_GROUP) ** -0.5),
        "ssm_b_im": nrm(ks[10], (DEPTH, N_SSM_GROUPS, SSM_STATE, SSM_GROUP), (2 * SSM_GROUP) ** -0.5),
        "ssm_c_re": nrm(ks[11], (DEPTH, N_SSM_GROUPS, SSM_GROUP, SSM_STATE), SSM_STATE ** -0.5),
        "ssm_c_im": nrm(ks[12], (DEPTH, N_SSM_GROUPS, SSM_GROUP, SSM_STATE), SSM_STATE ** -0.5),
        "ssm_d": nrm(ks[13], (DEPTH, D_SSM), 1.0),
        "w_glu": nrm(ks[14], (DEPTH, D_SSM, 2 * D_MODEL), D_SSM ** -0.5),
        "w_o": nrm(ks[15], (DEPTH, D_MODEL, D_MODEL), D_MODEL ** -0.5),
        "norm_ffn": 1.0 + nrm(ks[16], (DEPTH, D_MODEL), 0.02),
        "dense_w_gate": nrm(ks[17], (N_DENSE, D_MODEL, D_FF), D_MODEL ** -0.5),
        "dense_w_up": nrm(ks[18], (N_DENSE, D_MODEL, D_FF), D_MODEL ** -0.5),
        "dense_w_down": nrm(ks[19], (N_DENSE, D_FF, D_MODEL), D_FF ** -0.5),
        "router_w": nrm(ks[20], (N_MOE, D_MODEL, N_EXPERTS), D_MODEL ** -0.5),
        "router_b": nrm(ks[21], (N_MOE, N_EXPERTS), 0.01),
        "moe_w_gate": nrm(ks[22], (N_MOE, N_EXPERTS, D_MODEL, D_FF), D_MODEL ** -0.5),
        "moe_w_up": nrm(ks[23], (N_MOE, N_EXPERTS, D_MODEL, D_FF), D_MODEL ** -0.5),
        "moe_w_down": nrm(ks[24], (N_MOE, N_EXPERTS, D_FF, D_MODEL), D_FF ** -0.5),
        "norm_final": 1.0 + nrm(ks[25], (D_MODEL,), 0.02),
    }


def reference(x, norm_mix, w_in, b_in, conv_w, w_conv_out, ssm_a_re, ssm_a_im, ssm_log_dt,
              ssm_b_re, ssm_b_im, ssm_c_re, ssm_c_im, ssm_d, w_glu, w_o, norm_ffn,
              dense_w_gate, dense_w_up, dense_w_down, router_w, router_b,
              moe_w_gate, moe_w_up, moe_w_down, norm_final):
    o_b = 0
    o_c = o_b + D_CONV
    o_v = o_c + D_CONV
    o_u = o_v + D_CONV
    o_g = o_u + D_SSM
    for layer in range(DEPTH):
        h = rmsnorm(x, norm_mix[layer])
        proj = h @ w_in[layer] + b_in[layer]
        b_gate = proj[..., o_b:o_c]
        c_gate = proj[..., o_c:o_v]
        v = proj[..., o_v:o_u]
        u = proj[..., o_u:o_g]
        g_conv = jax.nn.sigmoid(proj[..., o_g:o_g + D_MODEL])
        g_ssm = jax.nn.sigmoid(proj[..., o_g + D_MODEL:o_g + 2 * D_MODEL])
        y_conv = short_conv_branch(b_gate, c_gate, v, conv_w[layer], w_conv_out[layer])
        y_ssm = s5_branch(u, ssm_a_re[layer], ssm_a_im[layer], ssm_log_dt[layer],
                          ssm_b_re[layer], ssm_b_im[layer], ssm_c_re[layer], ssm_c_im[layer],
                          ssm_d[layer], w_glu[layer])
        x = x + (g_conv * y_conv + g_ssm * y_ssm) @ w_o[layer]
        h = rmsnorm(x, norm_ffn[layer])
        if layer % 2 == 0:
            i = layer // 2
            f = swiglu(h, dense_w_gate[i], dense_w_up[i], dense_w_down[i])
        else:
            i = layer // 2
            f = moe_swiglu(h, router_w[i], router_b[i], moe_w_gate[i], moe_w_up[i], moe_w_down[i])
        x = x + f
    return rmsnorm(x, norm_final)
```

```python
import functools
import math

import jax
import jax.numpy as jnp
from jax import lax
from jax.experimental import pallas as pl
from jax.experimental.pallas import tpu as pltpu

F32 = jnp.float32
BF16 = jnp.bfloat16

RMS_EPS = 1e-6
TOP_K = 2
V7X_LANES = 128
V7X_SUBLANES = 8
V7X_VMEM_LIMIT_BYTES = 60 * 1024 * 1024

MIX_TIME_STEPS = 64
SCAN_COLS = 1024
FFN_ROWS = 1024
FFN_COLS = 512
MOE_ROWS = 512
DISPATCH_TOKENS = 1024
COMBINE_TOKENS = 256


def _const_spec(shape):
    zeros = (0,) * len(shape)
    return pl.BlockSpec(shape, lambda *_: zeros, pipeline_mode=pl.Buffered(1))


def _rms(xf, g):
    return xf * lax.rsqrt(jnp.mean(xf * xf, axis=-1, keepdims=True) + RMS_EPS) * g


def _dot(a, b):
    return jnp.dot(a, b, preferred_element_type=F32)


def _pack_bf16_pairs(h):
    half = h.shape[1] // 2
    bits = lax.bitcast_convert_type(h.astype(BF16).astype(F32), jnp.uint32)
    return (bits[:, half:] & jnp.uint32(0xFFFF0000)) | (bits[:, :half] >> 16)


def _unpack_bf16_pairs(w):
    lo = lax.bitcast_convert_type(w << 16, F32)
    hi = lax.bitcast_convert_type(w & jnp.uint32(0xFFFF0000), F32)
    return jnp.concatenate([lo, hi], axis=1).astype(BF16)


def _mixer_kernel(*refs, d_conv, d_ssm, n_state_cols, n_experts, with_router):
    (x_ref, gmix_ref, win_ref, bin_ref, convw_ref, wconv_ref, bmat_ref, are_ref, aim_ref,
     cmat_ref, dskip_ref, wglu_ref, wo_ref, gffn_ref) = refs[:14]
    refs = refs[14:]
    if with_router:
        rw_ref, rb_ref, tri_ref = refs[:3]
        xo_ref, h2_ref, route_ref, counts_ref = refs[3:7]
        ucv_ref, state_ref, s_ref, cnt_ref = refs[7:]
    else:
        xo_ref, h2_ref = refs[:2]
        ucv_ref, state_ref, s_ref = refs[2:]

    rows, d_model = x_ref.shape
    sub = V7X_SUBLANES
    half_cols = n_state_cols // 2
    step = pl.program_id(0)

    @pl.when(step == 0)
    def _():
        ucv_ref[0:2 * sub, :] = jnp.zeros((2 * sub, d_conv), F32)
        state_ref[...] = jnp.zeros_like(state_ref)
        if with_router:
            cnt_ref[...] = jnp.zeros_like(cnt_ref)

    x = x_ref[...]
    h = _rms(x, gmix_ref[...]).astype(BF16)

    def proj(lo, width):
        return _dot(h, win_ref[:, lo:lo + width]) + bin_ref[:, lo:lo + width]

    o_c, o_v, o_u = d_conv, 2 * d_conv, 3 * d_conv
    o_g = o_u + d_ssm

    ucv_ref[2 * sub:, :] = proj(o_c, d_conv) * proj(o_v, d_conv)
    conv = (convw_ref[0:1, :] * ucv_ref[0:rows, :]
            + convw_ref[1:2, :] * ucv_ref[sub:rows + sub, :]
            + convw_ref[2:3, :] * ucv_ref[2 * sub:, :])
    ucv_ref[0:2 * sub, :] = ucv_ref[rows:rows + 2 * sub, :]
    y_conv = _dot((proj(0, d_conv) * conv).astype(BF16), wconv_ref[...])

    u = proj(o_u, d_ssm)
    ub = u.astype(BF16)
    n_half = n_state_cols // half_cols
    u_cols = d_ssm // n_half
    for hh in range(n_half):
        s_ref[:, hh * half_cols:(hh + 1) * half_cols] = _dot(
            ub[:, hh * u_cols:(hh + 1) * u_cols], bmat_ref[hh])

    part = half_cols // 2
    for hh in range(n_half):
        for c0 in range(0, part, SCAN_COLS):
            re0 = hh * half_cols + c0
            im0 = re0 + part
            a_re = are_ref[:, hh * part + c0:hh * part + c0 + SCAN_COLS]
            a_im = aim_ref[:, hh * part + c0:hh * part + c0 + SCAN_COLS]

            def scan_step(t, carry, re0=re0, im0=im0, a_re=a_re, a_im=a_im):
                s_re, s_im = carry
                r0 = pl.multiple_of(t * sub, sub)
                b_re = s_ref[pl.ds(r0, sub), re0:re0 + SCAN_COLS]
                b_im = s_ref[pl.ds(r0, sub), im0:im0 + SCAN_COLS]
                n_re = a_re * s_re - a_im * s_im + b_re
                n_im = a_re * s_im + a_im * s_re + b_im
                s_ref[pl.ds(r0, sub), re0:re0 + SCAN_COLS] = n_re
                s_ref[pl.ds(r0, sub), im0:im0 + SCAN_COLS] = n_im
                return n_re, n_im

            init = (state_ref[:, re0:re0 + SCAN_COLS], state_ref[:, im0:im0 + SCAN_COLS])
            s_re, s_im = lax.fori_loop(0, rows // sub, scan_step, init)
            state_ref[:, re0:re0 + SCAN_COLS] = s_re
            state_ref[:, im0:im0 + SCAN_COLS] = s_im

    y = jnp.concatenate(
        [_dot(s_ref[:, hh * half_cols:(hh + 1) * half_cols].astype(BF16), cmat_ref[hh])
         for hh in range(n_half)], axis=1)
    y = jax.nn.gelu(y + dskip_ref[...] * u, approximate=True).astype(BF16)
    z = _dot(y, wglu_ref[...])
    y_ssm = z[:, :d_model] * jax.nn.sigmoid(z[:, d_model:])

    mix = (jax.nn.sigmoid(proj(o_g, d_model)) * y_conv
           + jax.nn.sigmoid(proj(o_g + d_model, d_model)) * y_ssm)
    xn = x + _dot(mix.astype(BF16), wo_ref[...])
    xo_ref[...] = xn
    h2 = _rms(xn, gffn_ref[...])

    if not with_router:
        h2_ref[...] = h2.astype(BF16)
        return

    h2_ref[...] = _pack_bf16_pairs(h2)

    lanes = V7X_LANES
    lane = lax.broadcasted_iota(jnp.int32, (rows, lanes), 1).astype(F32)
    logits = _dot(h2.astype(BF16), rw_ref[...]) + rb_ref[...]
    lg = jnp.where(lane < n_experts, logits, -jnp.inf)
    m1 = jnp.max(lg, axis=1, keepdims=True)
    i1 = jnp.min(jnp.where(lg == m1, lane, float(lanes)), axis=1, keepdims=True)
    lg2 = jnp.where(lane == i1, -jnp.inf, lg)
    m2 = jnp.max(lg2, axis=1, keepdims=True)
    i2 = jnp.min(jnp.where(lg2 == m2, lane, float(lanes)), axis=1, keepdims=True)
    e2 = jnp.exp(m2 - m1)
    w1 = 1.0 / (1.0 + e2)
    w2 = e2 / (1.0 + e2)
    onehot = jnp.where((lane == i1) | (lane == i2), 1.0, 0.0)
    before = _dot(tri_ref[...], onehot.astype(BF16)) + cnt_ref[0:1, :]
    r1 = jnp.sum(jnp.where(lane == i1, before, 0.0), axis=1, keepdims=True)
    r2 = jnp.sum(jnp.where(lane == i2, before, 0.0), axis=1, keepdims=True)
    cnt_ref[...] = cnt_ref[...] + jnp.sum(onehot, axis=0, keepdims=True)
    counts_ref[...] = cnt_ref[...]
    route = jnp.where(lane == 0, i1, 0.0)
    route = jnp.where(lane == 1, i2, route)
    route = jnp.where(lane == 2, w1, route)
    route = jnp.where(lane == 3, w2, route)
    route = jnp.where(lane == 4, r1, route)
    route = jnp.where(lane == 5, r2, route)
    route_ref[...] = route


def _mixer(x_rows, p, router):
    t_rows, d_model = x_rows.shape
    rows = MIX_TIME_STEPS * V7X_SUBLANES
    assert t_rows % rows == 0
    d_conv = p["w_conv_out"].shape[0]
    d_ssm = p["w_glu"].shape[0]
    n_state_cols = p["a_re"].shape[1] * 2
    with_router = router is not None
    n_experts = router["n_experts"] if with_router else 0

    row_spec = lambda width: pl.BlockSpec((rows, width), lambda i: (i, 0))
    consts = [p["norm_mix"], p["w_in"], p["b_in"], p["conv_w"], p["w_conv_out"], p["bmat"],
              p["a_re"], p["a_im"], p["cmat"], p["ssm_d"], p["w_glu"], p["w_o"], p["norm_ffn"]]
    out_shape = [jax.ShapeDtypeStruct((t_rows, d_model), F32)]
    out_specs = [row_spec(d_model)]
    scratch = [pltpu.VMEM((rows + 2 * V7X_SUBLANES, d_conv), F32),
               pltpu.VMEM((V7X_SUBLANES, n_state_cols), F32),
               pltpu.VMEM((rows, n_state_cols), F32)]
    if with_router:
        consts += [router["w"], router["b"], router["tri"]]
        out_shape += [jax.ShapeDtypeStruct((t_rows, d_model // 2), jnp.uint32),
                      jax.ShapeDtypeStruct((t_rows, V7X_LANES), F32),
                      jax.ShapeDtypeStruct((V7X_SUBLANES, V7X_LANES), F32)]
        out_specs += [row_spec(d_model // 2), row_spec(V7X_LANES),
                      pl.BlockSpec((V7X_SUBLANES, V7X_LANES), lambda i: (0, 0))]
        scratch += [pltpu.VMEM((V7X_SUBLANES, V7X_LANES), F32)]
    else:
        out_shape += [jax.ShapeDtypeStruct((t_rows, d_model), BF16)]
        out_specs += [row_spec(d_model)]

    kern = functools.partial(_mixer_kernel, d_conv=d_conv, d_ssm=d_ssm,
                             n_state_cols=n_state_cols, n_experts=n_experts,
                             with_router=with_router)
    return pl.pallas_call(
        kern,
        grid=(t_rows // rows,),
        in_specs=[row_spec(d_model)] + [_const_spec(c.shape) for c in consts],
        out_specs=out_specs,
        out_shape=out_shape,
        scratch_shapes=scratch,
        compiler_params=pltpu.CompilerParams(
            dimension_semantics=("arbitrary",), vmem_limit_bytes=V7X_VMEM_LIMIT_BYTES),
        name="mixer_router" if with_router else "mixer",
    )(x_rows, *consts)


def _swiglu_partial(hb, wg_ref, wu_ref, wd_ref):
    g = _dot(hb, wg_ref[...])
    a = (jax.nn.silu(g) * _dot(hb, wu_ref[...])).astype(BF16)
    return _dot(a, wd_ref[...])


def _dense_ffn_kernel(x_ref, h_ref, wg_ref, wu_ref, wd_ref, gfin_ref, o_ref, acc_ref, *,
                      final_norm):
    f = pl.program_id(1)

    @pl.when(f == 0)
    def _():
        acc_ref[...] = jnp.zeros_like(acc_ref)

    acc_ref[...] += _swiglu_partial(h_ref[...], wg_ref, wu_ref, wd_ref)

    @pl.when(f == pl.num_programs(1) - 1)
    def _():
        out = x_ref[...] + acc_ref[...]
        o_ref[...] = _rms(out, gfin_ref[...]) if final_norm else out


def _dense_ffn(x_rows, h2, w_gate, w_up, w_down, g_final, final_norm):
    t_rows, d_model = x_rows.shape
    d_ff = w_gate.shape[1]
    tm, tf = FFN_ROWS, FFN_COLS
    assert t_rows % tm == 0 and d_ff % tf == 0
    return pl.pallas_call(
        functools.partial(_dense_ffn_kernel, final_norm=final_norm),
        grid=(t_rows // tm, d_ff // tf),
        in_specs=[pl.BlockSpec((tm, d_model), lambda i, f: (i, 0)),
                  pl.BlockSpec((tm, d_model), lambda i, f: (i, 0)),
                  pl.BlockSpec((d_model, tf), lambda i, f: (0, f)),
                  pl.BlockSpec((d_model, tf), lambda i, f: (0, f)),
                  pl.BlockSpec((tf, d_model), lambda i, f: (f, 0)),
                  pl.BlockSpec((1, d_model), lambda i, f: (0, 0))],
        out_specs=pl.BlockSpec((tm, d_model), lambda i, f: (i, 0)),
        out_shape=jax.ShapeDtypeStruct((t_rows, d_model), F32),
        scratch_shapes=[pltpu.VMEM((tm, d_model), F32)],
        compiler_params=pltpu.CompilerParams(
            dimension_semantics=("arbitrary", "arbitrary"),
            vmem_limit_bytes=V7X_VMEM_LIMIT_BYTES),
        name="dense_swiglu",
    )(x_rows, h2, w_gate, w_up, w_down, g_final)


def _grouped_ffn_kernel(tile_e_ref, nvalid_ref, hs_ref, wg_ref, wu_ref, wd_ref, y_ref,
                        hb_ref, acc_ref):
    i = pl.program_id(0)
    f = pl.program_id(1)
    valid = i < nvalid_ref[0]

    @pl.when(valid & (f == 0))
    def _():
        hb_ref[...] = _unpack_bf16_pairs(hs_ref[...])
        acc_ref[...] = jnp.zeros_like(acc_ref)

    @pl.when(valid)
    def _():
        acc_ref[...] += _swiglu_partial(hb_ref[...], wg_ref, wu_ref, wd_ref)

    last = f == pl.num_programs(1) - 1

    @pl.when(valid & last)
    def _():
        y_ref[...] = acc_ref[...]

    @pl.when(jnp.logical_not(valid) & last)
    def _():
        y_ref[...] = jnp.zeros_like(y_ref)


def _grouped_ffn(h_sorted, tile_e, nvalid, w_gate, w_up, w_down):
    p_rows, half = h_sorted.shape
    d_model = 2 * half
    d_ff = w_gate.shape[2]
    tm, tf = MOE_ROWS, FFN_COLS
    n_f = d_ff // tf
    assert p_rows % tm == 0 and d_ff % tf == 0

    def row_map(i, f, tile_e, nvalid):
        return (jnp.minimum(i, nvalid[0] - 1), 0)

    def f_eff(i, f, nvalid):
        return jnp.where(i < nvalid[0], f, n_f - 1)

    grid_spec = pltpu.PrefetchScalarGridSpec(
        num_scalar_prefetch=2,
        grid=(p_rows // tm, n_f),
        in_specs=[pl.BlockSpec((tm, half), row_map),
                  pl.BlockSpec((None, d_model, tf),
                               lambda i, f, te, nv: (te[i], 0, f_eff(i, f, nv))),
                  pl.BlockSpec((None, d_model, tf),
                               lambda i, f, te, nv: (te[i], 0, f_eff(i, f, nv))),
                  pl.BlockSpec((None, tf, d_model),
                               lambda i, f, te, nv: (te[i], f_eff(i, f, nv), 0))],
        out_specs=pl.BlockSpec((tm, d_model), lambda i, f, te, nv: (i, 0)),
        scratch_shapes=[pltpu.VMEM((tm, d_model), BF16), pltpu.VMEM((tm, d_model), F32)],
    )
    return pl.pallas_call(
        _grouped_ffn_kernel,
        grid_spec=grid_spec,
        out_shape=jax.ShapeDtypeStruct((p_rows, d_model), F32),
        compiler_params=pltpu.CompilerParams(
            dimension_semantics=("arbitrary", "arbitrary"),
            vmem_limit_bytes=V7X_VMEM_LIMIT_BYTES),
        name="grouped_swiglu",
    )(tile_e, nvalid, h_sorted, w_gate, w_up, w_down)


def _dispatch_kernel(pos_ref, h_hbm, hs_in_hbm, hs_hbm, sem):
    del hs_in_hbm
    n_tok = pos_ref.shape[2] // TOP_K
    base = pl.program_id(0) * n_tok

    def row_copy(src_row, dst_row):
        return pltpu.make_async_copy(h_hbm.at[pl.ds(src_row, 1)],
                                     hs_hbm.at[pl.ds(dst_row, 1)], sem)

    def issue(j, _):
        for k in range(TOP_K):
            row_copy(base + j, pos_ref[0, 0, TOP_K * j + k]).start()
        return 0

    lax.fori_loop(0, n_tok, issue, 0)

    def drain(j, _):
        for k in range(TOP_K):
            row_copy(base + j, pos_ref[0, 0, TOP_K * j + k]).wait()
        return 0

    lax.fori_loop(0, n_tok, drain, 0)


def _dispatch(h_packed, pos, p_rows):
    t_rows, half = h_packed.shape
    tb = DISPATCH_TOKENS
    assert t_rows % tb == 0
    pos3 = pos.reshape(t_rows // tb, 1, TOP_K * tb)
    zeros = jnp.zeros((p_rows, half), h_packed.dtype)
    return pl.pallas_call(
        _dispatch_kernel,
        grid=(t_rows // tb,),
        in_specs=[pl.BlockSpec((1, 1, TOP_K * tb), lambda i: (i, 0, 0),
                               memory_space=pltpu.SMEM),
                  pl.BlockSpec(memory_space=pl.ANY),
                  pl.BlockSpec(memory_space=pl.ANY)],
        out_specs=pl.BlockSpec(memory_space=pl.ANY),
        out_shape=jax.ShapeDtypeStruct((p_rows, half), h_packed.dtype),
        scratch_shapes=[pltpu.SemaphoreType.DMA(())],
        input_output_aliases={2: 0},
        compiler_params=pltpu.CompilerParams(dimension_semantics=("arbitrary",),
                                             has_side_effects=True),
        name="moe_dispatch",
    )(pos3, h_packed, zeros)


def _combine_kernel(pos_ref, x_ref, route_ref, gfin_ref, y_hbm, o_ref, ybuf_ref, sem, *,
                    final_norm):
    n_tok = x_ref.shape[0]

    def row_copy(j, k):
        return pltpu.make_async_copy(y_hbm.at[pl.ds(pos_ref[0, 0, TOP_K * j + k], 1)],
                                     ybuf_ref.at[k, pl.ds(j, 1)], sem)

    def issue(j, _):
        for k in range(TOP_K):
            row_copy(j, k).start()
        return 0

    lax.fori_loop(0, n_tok, issue, 0)

    def drain(j, _):
        for k in range(TOP_K):
            row_copy(j, k).wait()
        return 0

    lax.fori_loop(0, n_tok, drain, 0)

    route = route_ref[...]
    out = x_ref[...] + (route[:, 2:3] * ybuf_ref[0] + route[:, 3:4] * ybuf_ref[1])
    o_ref[...] = _rms(out, gfin_ref[...]) if final_norm else out


def _combine(x_rows, route, pos, y_sorted, g_final, final_norm):
    t_rows, d_model = x_rows.shape
    tb = COMBINE_TOKENS
    assert t_rows % tb == 0
    pos3 = pos.reshape(t_rows // tb, 1, TOP_K * tb)
    return pl.pallas_call(
        functools.partial(_combine_kernel, final_norm=final_norm),
        grid=(t_rows // tb,),
        in_specs=[pl.BlockSpec((1, 1, TOP_K * tb), lambda i: (i, 0, 0),
                               memory_space=pltpu.SMEM),
                  pl.BlockSpec((tb, d_model), lambda i: (i, 0)),
                  pl.BlockSpec((tb, V7X_LANES), lambda i: (i, 0)),
                  pl.BlockSpec((1, d_model), lambda i: (0, 0)),
                  pl.BlockSpec(memory_space=pl.ANY)],
        out_specs=pl.BlockSpec((tb, d_model), lambda i: (i, 0)),
        out_shape=jax.ShapeDtypeStruct((t_rows, d_model), F32),
        scratch_shapes=[pltpu.VMEM((TOP_K, tb, d_model), F32), pltpu.SemaphoreType.DMA(())],
        compiler_params=pltpu.CompilerParams(dimension_semantics=("arbitrary",)),
        name="moe_combine",
    )(pos3, x_rows, route, g_final, y_sorted)


def _moe(x_rows, h_packed, route, counts, w_gate, w_up, w_down, g_final, final_norm):
    t_rows = x_rows.shape[0]
    n_experts = w_gate.shape[0]
    tm = MOE_ROWS
    p_rows = TOP_K * t_rows + n_experts * tm
    n_tiles = p_rows // tm

    idx = route[:, 0:TOP_K].astype(jnp.int32)
    rank = route[:, 4:4 + TOP_K].astype(jnp.int32)
    cnt = counts[0, :n_experts].astype(jnp.int32)
    tiles = (cnt + tm - 1) // tm
    tile_end = jnp.cumsum(tiles)
    offset = (tile_end - tiles) * tm
    pos = offset[idx] + rank
    nvalid = tile_end[-1:]
    tile_id = jnp.minimum(jnp.arange(n_tiles, dtype=jnp.int32), nvalid[0] - 1)
    tile_e = jnp.minimum(jnp.searchsorted(tile_end, tile_id, side="right"),
                         n_experts - 1).astype(jnp.int32)

    h_sorted = _dispatch(h_packed, pos, p_rows)
    y_sorted = _grouped_ffn(h_sorted, tile_e, nvalid.astype(jnp.int32), w_gate, w_up, w_down)
    return _combine(x_rows, route, pos, y_sorted, g_final, final_norm)


def _ssm_matrices(a_re, a_im, log_dt, b_re, b_im, c_re, c_im):
    n_groups, n_state = a_re.shape
    grp = b_re.shape[2]
    dt = jnp.exp(log_dt)[:, None]
    mag = jnp.exp(a_re * dt)
    abar_re = mag * jnp.cos(a_im * dt)
    abar_im = mag * jnp.sin(a_im * dt)
    den = a_re * a_re + a_im * a_im
    nr = abar_re - 1.0
    ni = abar_im
    coef_re = (nr * a_re + ni * a_im) / den
    coef_im = (ni * a_re - nr * a_im) / den
    bb_re = coef_re[..., None] * b_re - coef_im[..., None] * b_im
    bb_im = coef_re[..., None] * b_im + coef_im[..., None] * b_re

    n_half = 2
    gh = n_groups // n_half
    eye = jnp.eye(gh, dtype=F32)

    def b_block(bb):
        return jnp.einsum("gnc,gh->gchn", bb, eye).reshape(gh * grp, gh * n_state)

    def c_block(cc):
        return jnp.einsum("gcn,gh->gnhc", cc, eye).reshape(gh * n_state, gh * grp)

    bmat, cmat, are, aim = [], [], [], []
    for hh in range(n_half):
        sl = slice(hh * gh, (hh + 1) * gh)
        bmat.append(jnp.concatenate([b_block(bb_re[sl]), b_block(bb_im[sl])], axis=1))
        cmat.append(jnp.concatenate([c_block(c_re[sl]), -c_block(c_im[sl])], axis=0))
        are.append(abar_re[sl].reshape(-1))
        aim.append(abar_im[sl].reshape(-1))
    bcast = lambda v: jnp.broadcast_to(jnp.concatenate(v)[None, :],
                                       (V7X_SUBLANES, n_groups * n_state))
    return (jnp.stack(bmat).astype(BF16), jnp.stack(cmat).astype(BF16), bcast(are), bcast(aim))


def kernel(x, norm_mix, w_in, b_in, conv_w, w_conv_out, ssm_a_re, ssm_a_im, ssm_log_dt,
           ssm_b_re, ssm_b_im, ssm_c_re, ssm_c_im, ssm_d, w_glu, w_o, norm_ffn,
           dense_w_gate, dense_w_up, dense_w_down, router_w, router_b,
           moe_w_gate, moe_w_up, moe_w_down, norm_final):
    bsz, seq, d_model = x.shape
    assert bsz == V7X_SUBLANES, "one time step must be one 8-row sublane tile"
    depth = w_in.shape[0]
    n_experts = router_w.shape[-1]
    t_rows = bsz * seq
    xr = jnp.transpose(x, (1, 0, 2)).reshape(t_rows, d_model)
    g_final = norm_final.reshape(1, d_model)
    rows = MIX_TIME_STEPS * V7X_SUBLANES
    tri = jnp.tril(jnp.ones((rows, rows), F32), -1).astype(BF16)

    for layer in range(depth):
        bmat, cmat, are, aim = _ssm_matrices(
            ssm_a_re[layer], ssm_a_im[layer], ssm_log_dt[layer], ssm_b_re[layer],
            ssm_b_im[layer], ssm_c_re[layer], ssm_c_im[layer])
        p = dict(norm_mix=norm_mix[layer].reshape(1, -1), w_in=w_in[layer].astype(BF16),
                 b_in=b_in[layer].reshape(1, -1), conv_w=conv_w[layer],
                 w_conv_out=w_conv_out[layer].astype(BF16), bmat=bmat, cmat=cmat,
                 a_re=are, a_im=aim, ssm_d=ssm_d[layer].reshape(1, -1),
                 w_glu=w_glu[layer].astype(BF16), w_o=w_o[layer].astype(BF16),
                 norm_ffn=norm_ffn[layer].reshape(1, -1))
        last = layer == depth - 1
        i = layer // 2
        if layer % 2 == 0:
            xr, h2 = _mixer(xr, p, None)
            xr = _dense_ffn(xr, h2, dense_w_gate[i].astype(BF16), dense_w_up[i].astype(BF16),
                            dense_w_down[i].astype(BF16), g_final, last)
        else:
            rw = jnp.zeros((d_model, V7X_LANES), F32).at[:, :n_experts].set(router_w[i])
            rb = jnp.zeros((1, V7X_LANES), F32).at[0, :n_experts].set(router_b[i])
            router = dict(w=rw.astype(BF16), b=rb, tri=tri, n_experts=n_experts)
            xr, h_packed, route, counts = _mixer(xr, p, router)
            xr = _moe(xr, h_packed, route, counts, moe_w_gate[i].astype(BF16),
                      moe_w_up[i].astype(BF16), moe_w_down[i].astype(BF16), g_final, last)
    return jnp.transpose(xr.reshape(seq, bsz, d_model), (1, 0, 2))
```

```python
import functools
import math

import jax
import jax.numpy as jnp
from jax import lax
from jax.experimental import pallas as pl
from jax.experimental.pallas import tpu as pltpu

F32 = jnp.float32
BF16 = jnp.bfloat16

RMS_EPS = 1e-6
TOP_K = 2
V7X_LANES = 128
V7X_SUBLANES = 8
V7X_VMEM_LIMIT_BYTES = 60 * 1024 * 1024

MIX_TIME_STEPS = 64
SCAN_COLS = 1024
FFN_ROWS = 1024
FFN_COLS = 512
MOE_ROWS = 512
DISPATCH_TOKENS = 1024
COMBINE_TOKENS = 256
DMA_UNROLL = 8


def _const_spec(shape):
    zeros = (0,) * len(shape)
    return pl.BlockSpec(shape, lambda *_: zeros, pipeline_mode=pl.Buffered(1))


def _rms(xf, g):
    return xf * lax.rsqrt(jnp.mean(xf * xf, axis=-1, keepdims=True) + RMS_EPS) * g


def _dot(a, b):
    return jnp.dot(a, b, preferred_element_type=F32)


def _mixer_kernel(*refs, d_conv, d_ssm, n_state_cols, n_experts, with_router):
    (x_ref, gmix_ref, win_ref, bin_ref, convw_ref, wconv_ref, bmat_ref, are_ref, aim_ref,
     cmat_ref, dskip_ref, wglu_ref, wo_ref, gffn_ref) = refs[:14]
    refs = refs[14:]
    if with_router:
        rw_ref, rb_ref, tri_ref = refs[:3]
        xo_ref, h2_ref, route_ref, counts_ref = refs[3:7]
        ucv_ref, state_ref, s_ref, cnt_ref = refs[7:]
    else:
        xo_ref, h2_ref = refs[:2]
        ucv_ref, state_ref, s_ref = refs[2:]

    rows, d_model = x_ref.shape
    sub = V7X_SUBLANES
    half_cols = n_state_cols // 2
    step = pl.program_id(0)

    @pl.when(step == 0)
    def _():
        ucv_ref[0:2 * sub, :] = jnp.zeros((2 * sub, d_conv), F32)
        state_ref[...] = jnp.zeros_like(state_ref)
        if with_router:
            cnt_ref[...] = jnp.zeros_like(cnt_ref)

    x = x_ref[...]
    h = _rms(x, gmix_ref[...]).astype(BF16)

    def proj(lo, width):
        return _dot(h, win_ref[:, lo:lo + width]) + bin_ref[:, lo:lo + width]

    o_c, o_v, o_u = d_conv, 2 * d_conv, 3 * d_conv
    o_g = o_u + d_ssm

    ucv_ref[2 * sub:, :] = proj(o_c, d_conv) * proj(o_v, d_conv)
    conv = (convw_ref[0:1, :] * ucv_ref[0:rows, :]
            + convw_ref[1:2, :] * ucv_ref[sub:rows + sub, :]
            + convw_ref[2:3, :] * ucv_ref[2 * sub:, :])
    ucv_ref[0:2 * sub, :] = ucv_ref[rows:rows + 2 * sub, :]
    y_conv = _dot((proj(0, d_conv) * conv).astype(BF16), wconv_ref[...])

    u = proj(o_u, d_ssm)
    ub = u.astype(BF16)
    n_half = n_state_cols // half_cols
    u_cols = d_ssm // n_half
    for hh in range(n_half):
        s_ref[:, hh * half_cols:(hh + 1) * half_cols] = _dot(
            ub[:, hh * u_cols:(hh + 1) * u_cols], bmat_ref[hh])

    part = half_cols // 2
    for hh in range(n_half):
        for c0 in range(0, part, SCAN_COLS):
            re0 = hh * half_cols + c0
            im0 = re0 + part
            a_re = are_ref[:, hh * part + c0:hh * part + c0 + SCAN_COLS]
            a_im = aim_ref[:, hh * part + c0:hh * part + c0 + SCAN_COLS]

            def scan_step(t, carry, re0=re0, im0=im0, a_re=a_re, a_im=a_im):
                s_re, s_im = carry
                r0 = pl.multiple_of(t * sub, sub)
                b_re = s_ref[pl.ds(r0, sub), re0:re0 + SCAN_COLS]
                b_im = s_ref[pl.ds(r0, sub), im0:im0 + SCAN_COLS]
                n_re = a_re * s_re - a_im * s_im + b_re
                n_im = a_re * s_im + a_im * s_re + b_im
                s_ref[pl.ds(r0, sub), re0:re0 + SCAN_COLS] = n_re
                s_ref[pl.ds(r0, sub), im0:im0 + SCAN_COLS] = n_im
                return n_re, n_im

            init = (state_ref[:, re0:re0 + SCAN_COLS], state_ref[:, im0:im0 + SCAN_COLS])
            s_re, s_im = lax.fori_loop(0, rows // sub, scan_step, init)
            state_ref[:, re0:re0 + SCAN_COLS] = s_re
            state_ref[:, im0:im0 + SCAN_COLS] = s_im

    y = jnp.concatenate(
        [_dot(s_ref[:, hh * half_cols:(hh + 1) * half_cols].astype(BF16), cmat_ref[hh])
         for hh in range(n_half)], axis=1)
    y = jax.nn.gelu(y + dskip_ref[...] * u, approximate=True).astype(BF16)
    z = _dot(y, wglu_ref[...])
    y_ssm = z[:, :d_model] * jax.nn.sigmoid(z[:, d_model:])

    mix = (jax.nn.sigmoid(proj(o_g, d_model)) * y_conv
           + jax.nn.sigmoid(proj(o_g + d_model, d_model)) * y_ssm)
    xn = x + _dot(mix.astype(BF16), wo_ref[...])
    xo_ref[...] = xn
    h2 = _rms(xn, gffn_ref[...])

    if not with_router:
        h2_ref[...] = h2.astype(BF16)
        return

    h2_ref[...] = h2

    lanes = V7X_LANES
    lane = lax.broadcasted_iota(jnp.int32, (rows, lanes), 1).astype(F32)
    logits = _dot(h2.astype(BF16), rw_ref[...]) + rb_ref[...]
    lg = jnp.where(lane < n_experts, logits, -jnp.inf)
    m1 = jnp.max(lg, axis=1, keepdims=True)
    i1 = jnp.min(jnp.where(lg == m1, lane, float(lanes)), axis=1, keepdims=True)
    lg2 = jnp.where(lane == i1, -jnp.inf, lg)
    m2 = jnp.max(lg2, axis=1, keepdims=True)
    i2 = jnp.min(jnp.where(lg2 == m2, lane, float(lanes)), axis=1, keepdims=True)
    e2 = jnp.exp(m2 - m1)
    w1 = 1.0 / (1.0 + e2)
    w2 = e2 / (1.0 + e2)
    onehot = jnp.where((lane == i1) | (lane == i2), 1.0, 0.0)
    before = _dot(tri_ref[...], onehot.astype(BF16)) + cnt_ref[0:1, :]
    r1 = jnp.sum(jnp.where(lane == i1, before, 0.0), axis=1, keepdims=True)
    r2 = jnp.sum(jnp.where(lane == i2, before, 0.0), axis=1, keepdims=True)
    cnt_ref[...] = cnt_ref[...] + jnp.sum(onehot, axis=0, keepdims=True)
    counts_ref[...] = cnt_ref[...]
    route = jnp.where(lane == 0, i1, 0.0)
    route = jnp.where(lane == 1, i2, route)
    route = jnp.where(lane == 2, w1, route)
    route = jnp.where(lane == 3, w2, route)
    route = jnp.where(lane == 4, r1, route)
    route = jnp.where(lane == 5, r2, route)
    route_ref[...] = route


def _mixer(x_rows, p, router):
    t_rows, d_model = x_rows.shape
    rows = MIX_TIME_STEPS * V7X_SUBLANES
    assert t_rows % rows == 0
    d_conv = p["w_conv_out"].shape[0]
    d_ssm = p["w_glu"].shape[0]
    n_state_cols = p["a_re"].shape[1] * 2
    with_router = router is not None
    n_experts = router["n_experts"] if with_router else 0

    row_spec = lambda width: pl.BlockSpec((rows, width), lambda i: (i, 0))
    consts = [p["norm_mix"], p["w_in"], p["b_in"], p["conv_w"], p["w_conv_out"], p["bmat"],
              p["a_re"], p["a_im"], p["cmat"], p["ssm_d"], p["w_glu"], p["w_o"], p["norm_ffn"]]
    out_shape = [jax.ShapeDtypeStruct((t_rows, d_model), F32)]
    out_specs = [row_spec(d_model)]
    scratch = [pltpu.VMEM((rows + 2 * V7X_SUBLANES, d_conv), F32),
               pltpu.VMEM((V7X_SUBLANES, n_state_cols), F32),
               pltpu.VMEM((rows, n_state_cols), F32)]
    if with_router:
        consts += [router["w"], router["b"], router["tri"]]
        out_shape += [jax.ShapeDtypeStruct((t_rows, d_model), F32),
                      jax.ShapeDtypeStruct((t_rows, V7X_LANES), F32),
                      jax.ShapeDtypeStruct((V7X_SUBLANES, V7X_LANES), F32)]
        out_specs += [row_spec(d_model), row_spec(V7X_LANES),
                      pl.BlockSpec((V7X_SUBLANES, V7X_LANES), lambda i: (0, 0))]
        scratch += [pltpu.VMEM((V7X_SUBLANES, V7X_LANES), F32)]
    else:
        out_shape += [jax.ShapeDtypeStruct((t_rows, d_model), BF16)]
        out_specs += [row_spec(d_model)]

    kern = functools.partial(_mixer_kernel, d_conv=d_conv, d_ssm=d_ssm,
                             n_state_cols=n_state_cols, n_experts=n_experts,
                             with_router=with_router)
    return pl.pallas_call(
        kern,
        grid=(t_rows // rows,),
        in_specs=[row_spec(d_model)] + [_const_spec(c.shape) for c in consts],
        out_specs=out_specs,
        out_shape=out_shape,
        scratch_shapes=scratch,
        compiler_params=pltpu.CompilerParams(
            dimension_semantics=("arbitrary",), vmem_limit_bytes=V7X_VMEM_LIMIT_BYTES),
        name="mixer_router" if with_router else "mixer",
    )(x_rows, *consts)


def _swiglu_partial(hb, wg_ref, wu_ref, wd_ref):
    g = _dot(hb, wg_ref[...])
    a = (jax.nn.silu(g) * _dot(hb, wu_ref[...])).astype(BF16)
    return _dot(a, wd_ref[...])


def _dense_ffn_kernel(x_ref, h_ref, wg_ref, wu_ref, wd_ref, gfin_ref, o_ref, acc_ref, *,
                      final_norm):
    f = pl.program_id(1)

    @pl.when(f == 0)
    def _():
        acc_ref[...] = jnp.zeros_like(acc_ref)

    acc_ref[...] += _swiglu_partial(h_ref[...], wg_ref, wu_ref, wd_ref)

    @pl.when(f == pl.num_programs(1) - 1)
    def _():
        out = x_ref[...] + acc_ref[...]
        o_ref[...] = _rms(out, gfin_ref[...]) if final_norm else out


def _dense_ffn(x_rows, h2, w_gate, w_up, w_down, g_final, final_norm):
    t_rows, d_model = x_rows.shape
    d_ff = w_gate.shape[1]
    tm, tf = FFN_ROWS, FFN_COLS
    assert t_rows % tm == 0 and d_ff % tf == 0
    return pl.pallas_call(
        functools.partial(_dense_ffn_kernel, final_norm=final_norm),
        grid=(t_rows // tm, d_ff // tf),
        in_specs=[pl.BlockSpec((tm, d_model), lambda i, f: (i, 0)),
                  pl.BlockSpec((tm, d_model), lambda i, f: (i, 0)),
                  pl.BlockSpec((d_model, tf), lambda i, f: (0, f)),
                  pl.BlockSpec((d_model, tf), lambda i, f: (0, f)),
                  pl.BlockSpec((tf, d_model), lambda i, f: (f, 0)),
                  pl.BlockSpec((1, d_model), lambda i, f: (0, 0))],
        out_specs=pl.BlockSpec((tm, d_model), lambda i, f: (i, 0)),
        out_shape=jax.ShapeDtypeStruct((t_rows, d_model), F32),
        scratch_shapes=[pltpu.VMEM((tm, d_model), F32)],
        compiler_params=pltpu.CompilerParams(
            dimension_semantics=("arbitrary", "arbitrary"),
            vmem_limit_bytes=V7X_VMEM_LIMIT_BYTES),
        name="dense_swiglu",
    )(x_rows, h2, w_gate, w_up, w_down, g_final)


def _grouped_ffn_kernel(tile_e_ref, nvalid_ref, hs_ref, wg_ref, wu_ref, wd_ref, y_ref,
                        hb_ref, acc_ref):
    i = pl.program_id(0)
    f = pl.program_id(1)
    valid = i < nvalid_ref[0]

    @pl.when(valid & (f == 0))
    def _():
        hb_ref[...] = hs_ref[...].astype(BF16)
        acc_ref[...] = jnp.zeros_like(acc_ref)

    @pl.when(valid)
    def _():
        acc_ref[...] += _swiglu_partial(hb_ref[...], wg_ref, wu_ref, wd_ref)

    last = f == pl.num_programs(1) - 1

    @pl.when(valid & last)
    def _():
        y_ref[...] = acc_ref[...]

    @pl.when(jnp.logical_not(valid) & last)
    def _():
        y_ref[...] = jnp.zeros_like(y_ref)


def _grouped_ffn(h_sorted, tile_e, nvalid, w_gate, w_up, w_down):
    p_rows, d_model = h_sorted.shape
    d_ff = w_gate.shape[2]
    tm, tf = MOE_ROWS, FFN_COLS
    n_f = d_ff // tf
    assert p_rows % tm == 0 and d_ff % tf == 0

    def row_map(i, f, tile_e, nvalid):
        return (jnp.minimum(i, nvalid[0] - 1), 0)

    def f_eff(i, f, nvalid):
        return jnp.where(i < nvalid[0], f, n_f - 1)

    grid_spec = pltpu.PrefetchScalarGridSpec(
        num_scalar_prefetch=2,
        grid=(p_rows // tm, n_f),
        in_specs=[pl.BlockSpec((tm, d_model), row_map),
                  pl.BlockSpec((None, d_model, tf),
                               lambda i, f, te, nv: (te[i], 0, f_eff(i, f, nv))),
                  pl.BlockSpec((None, d_model, tf),
                               lambda i, f, te, nv: (te[i], 0, f_eff(i, f, nv))),
                  pl.BlockSpec((None, tf, d_model),
                               lambda i, f, te, nv: (te[i], f_eff(i, f, nv), 0))],
        out_specs=pl.BlockSpec((tm, d_model), lambda i, f, te, nv: (i, 0)),
        scratch_shapes=[pltpu.VMEM((tm, d_model), BF16), pltpu.VMEM((tm, d_model), F32)],
    )
    return pl.pallas_call(
        _grouped_ffn_kernel,
        grid_spec=grid_spec,
        out_shape=jax.ShapeDtypeStruct((p_rows, d_model), F32),
        compiler_params=pltpu.CompilerParams(
            dimension_semantics=("arbitrary", "arbitrary"),
            vmem_limit_bytes=V7X_VMEM_LIMIT_BYTES),
        name="grouped_swiglu",
    )(tile_e, nvalid, h_sorted, w_gate, w_up, w_down)


def _dispatch_kernel(pos_ref, h_ref, hs_in_hbm, hs_hbm, sem):
    del hs_in_hbm
    n_tok = h_ref.shape[0]

    def row_copy(j, k):
        return pltpu.make_async_copy(h_ref.at[pl.ds(j, 1)],
                                     hs_hbm.at[pl.ds(pos_ref[0, 0, TOP_K * j + k], 1)], sem)

    def issue(j, _):
        for k in range(TOP_K):
            row_copy(j, k).start()
        return 0

    lax.fori_loop(0, n_tok, issue, 0, unroll=DMA_UNROLL)

    def drain(j, _):
        for k in range(TOP_K):
            row_copy(j, k).wait()
        return 0

    lax.fori_loop(0, n_tok, drain, 0, unroll=DMA_UNROLL)


def _dispatch(h_packed, pos, p_rows):
    t_rows, half = h_packed.shape
    tb = DISPATCH_TOKENS
    assert t_rows % tb == 0
    pos3 = pos.reshape(t_rows // tb, 1, TOP_K * tb)
    zeros = jnp.zeros((p_rows, half), h_packed.dtype)
    return pl.pallas_call(
        _dispatch_kernel,
        grid=(t_rows // tb,),
        in_specs=[pl.BlockSpec((1, 1, TOP_K * tb), lambda i: (i, 0, 0),
                               memory_space=pltpu.SMEM),
                  pl.BlockSpec((tb, half), lambda i: (i, 0)),
                  pl.BlockSpec(memory_space=pl.ANY)],
        out_specs=pl.BlockSpec(memory_space=pl.ANY),
        out_shape=jax.ShapeDtypeStruct((p_rows, half), h_packed.dtype),
        scratch_shapes=[pltpu.SemaphoreType.DMA(())],
        input_output_aliases={2: 0},
        compiler_params=pltpu.CompilerParams(dimension_semantics=("arbitrary",),
                                             has_side_effects=True),
        name="moe_dispatch",
    )(pos3, h_packed, zeros)


def _combine_kernel(pos_ref, x_ref, route_ref, gfin_ref, y_hbm, o_ref, ybuf_ref, sem, *,
                    final_norm):
    n_tok = x_ref.shape[0]

    def row_copy(j, k):
        return pltpu.make_async_copy(y_hbm.at[pl.ds(pos_ref[0, 0, TOP_K * j + k], 1)],
                                     ybuf_ref.at[k, pl.ds(j, 1)], sem)

    def issue(j, _):
        for k in range(TOP_K):
            row_copy(j, k).start()
        return 0

    lax.fori_loop(0, n_tok, issue, 0, unroll=DMA_UNROLL)

    def drain(j, _):
        for k in range(TOP_K):
            row_copy(j, k).wait()
        return 0

    lax.fori_loop(0, n_tok, drain, 0, unroll=DMA_UNROLL)

    route = route_ref[...]
    out = x_ref[...] + (route[:, 2:3] * ybuf_ref[0] + route[:, 3:4] * ybuf_ref[1])
    o_ref[...] = _rms(out, gfin_ref[...]) if final_norm else out


def _combine(x_rows, route, pos, y_sorted, g_final, final_norm):
    t_rows, d_model = x_rows.shape
    tb = COMBINE_TOKENS
    assert t_rows % tb == 0
    pos3 = pos.reshape(t_rows // tb, 1, TOP_K * tb)
    return pl.pallas_call(
        functools.partial(_combine_kernel, final_norm=final_norm),
        grid=(t_rows // tb,),
        in_specs=[pl.BlockSpec((1, 1, TOP_K * tb), lambda i: (i, 0, 0),
                               memory_space=pltpu.SMEM),
                  pl.BlockSpec((tb, d_model), lambda i: (i, 0)),
                  pl.BlockSpec((tb, V7X_LANES), lambda i: (i, 0)),
                  pl.BlockSpec((1, d_model), lambda i: (0, 0)),
                  pl.BlockSpec(memory_space=pl.ANY)],
        out_specs=pl.BlockSpec((tb, d_model), lambda i: (i, 0)),
        out_shape=jax.ShapeDtypeStruct((t_rows, d_model), F32),
        scratch_shapes=[pltpu.VMEM((TOP_K, tb, d_model), F32), pltpu.SemaphoreType.DMA(())],
        compiler_params=pltpu.CompilerParams(dimension_semantics=("arbitrary",)),
        name="moe_combine",
    )(pos3, x_rows, route, g_final, y_sorted)


def _moe(x_rows, h_packed, route, counts, w_gate, w_up, w_down, g_final, final_norm):
    t_rows = x_rows.shape[0]
    n_experts = w_gate.shape[0]
    tm = MOE_ROWS
    p_rows = TOP_K * t_rows + n_experts * tm
    n_tiles = p_rows // tm

    idx = route[:, 0:TOP_K].astype(jnp.int32)
    rank = route[:, 4:4 + TOP_K].astype(jnp.int32)
    cnt = counts[0, :n_experts].astype(jnp.int32)
    tiles = (cnt + tm - 1) // tm
    tile_end = jnp.cumsum(tiles)
    offset = (tile_end - tiles) * tm
    pos = offset[idx] + rank
    nvalid = tile_end[-1:]
    tile_id = jnp.minimum(jnp.arange(n_tiles, dtype=jnp.int32), nvalid[0] - 1)
    tile_e = jnp.sum((tile_id[:, None] >= tile_end[None, :]).astype(jnp.int32), axis=1)
    tile_e = jnp.minimum(tile_e, n_experts - 1)

    h_sorted = _dispatch(h_packed, pos, p_rows)
    y_sorted = _grouped_ffn(h_sorted, tile_e, nvalid.astype(jnp.int32), w_gate, w_up, w_down)
    return _combine(x_rows, route, pos, y_sorted, g_final, final_norm)


def _ssm_matrices(a_re, a_im, log_dt, b_re, b_im, c_re, c_im):
    n_groups, n_state = a_re.shape
    grp = b_re.shape[2]
    dt = jnp.exp(log_dt)[:, None]
    mag = jnp.exp(a_re * dt)
    abar_re = mag * jnp.cos(a_im * dt)
    abar_im = mag * jnp.sin(a_im * dt)
    den = a_re * a_re + a_im * a_im
    nr = abar_re - 1.0
    ni = abar_im
    coef_re = (nr * a_re + ni * a_im) / den
    coef_im = (ni * a_re - nr * a_im) / den
    bb_re = coef_re[..., None] * b_re - coef_im[..., None] * b_im
    bb_im = coef_re[..., None] * b_im + coef_im[..., None] * b_re

    n_half = 2
    gh = n_groups // n_half
    eye = jnp.eye(gh, dtype=F32)

    def b_block(bb):
        return jnp.einsum("gnc,gh->gchn", bb, eye).reshape(gh * grp, gh * n_state)

    def c_block(cc):
        return jnp.einsum("gcn,gh->gnhc", cc, eye).reshape(gh * n_state, gh * grp)

    bmat, cmat, are, aim = [], [], [], []
    for hh in range(n_half):
        sl = slice(hh * gh, (hh + 1) * gh)
        bmat.append(jnp.concatenate([b_block(bb_re[sl]), b_block(bb_im[sl])], axis=1))
        cmat.append(jnp.concatenate([c_block(c_re[sl]), -c_block(c_im[sl])], axis=0))
        are.append(abar_re[sl].reshape(-1))
        aim.append(abar_im[sl].reshape(-1))
    bcast = lambda v: jnp.broadcast_to(jnp.concatenate(v)[None, :],
                                       (V7X_SUBLANES, n_groups * n_state))
    return (jnp.stack(bmat).astype(BF16), jnp.stack(cmat).astype(BF16), bcast(are), bcast(aim))


def kernel(x, norm_mix, w_in, b_in, conv_w, w_conv_out, ssm_a_re, ssm_a_im, ssm_log_dt,
           ssm_b_re, ssm_b_im, ssm_c_re, ssm_c_im, ssm_d, w_glu, w_o, norm_ffn,
           dense_w_gate, dense_w_up, dense_w_down, router_w, router_b,
           moe_w_gate, moe_w_up, moe_w_down, norm_final):
    bsz, seq, d_model = x.shape
    assert bsz == V7X_SUBLANES, "one time step must be one 8-row sublane tile"
    depth = w_in.shape[0]
    n_experts = router_w.shape[-1]
    t_rows = bsz * seq
    xr = jnp.transpose(x, (1, 0, 2)).reshape(t_rows, d_model)
    g_final = norm_final.reshape(1, d_model)
    rows = MIX_TIME_STEPS * V7X_SUBLANES
    tri = jnp.tril(jnp.ones((rows, rows), F32), -1).astype(BF16)

    for layer in range(depth):
        bmat, cmat, are, aim = _ssm_matrices(
            ssm_a_re[layer], ssm_a_im[layer], ssm_log_dt[layer], ssm_b_re[layer],
            ssm_b_im[layer], ssm_c_re[layer], ssm_c_im[layer])
        p = dict(norm_mix=norm_mix[layer].reshape(1, -1), w_in=w_in[layer].astype(BF16),
                 b_in=b_in[layer].reshape(1, -1), conv_w=conv_w[layer],
                 w_conv_out=w_conv_out[layer].astype(BF16), bmat=bmat, cmat=cmat,
                 a_re=are, a_im=aim, ssm_d=ssm_d[layer].reshape(1, -1),
                 w_glu=w_glu[layer].astype(BF16), w_o=w_o[layer].astype(BF16),
                 norm_ffn=norm_ffn[layer].reshape(1, -1))
        last = layer == depth - 1
        i = layer // 2
        if layer % 2 == 0:
            xr, h2 = _mixer(xr, p, None)
            xr = _dense_ffn(xr, h2, dense_w_gate[i].astype(BF16), dense_w_up[i].astype(BF16),
                            dense_w_down[i].astype(BF16), g_final, last)
        else:
            rw = jnp.zeros((d_model, V7X_LANES), F32).at[:, :n_experts].set(router_w[i])
            rb = jnp.zeros((1, V7X_LANES), F32).at[0, :n_experts].set(router_b[i])
            router = dict(w=rw.astype(BF16), b=rb, tri=tri, n_experts=n_experts)
            xr, h_packed, route, counts = _mixer(xr, p, router)
            xr = _moe(xr, h_packed, route, counts, moe_w_gate[i].astype(BF16),
                      moe_w_up[i].astype(BF16), moe_w_down[i].astype(BF16), g_final, last)
    return jnp.transpose(xr.reshape(seq, bsz, d_model), (1, 0, 2))
```

```python
import functools
import math

import jax
import jax.numpy as jnp
from jax import lax
from jax.experimental import pallas as pl
from jax.experimental.pallas import tpu as pltpu

F32 = jnp.float32
BF16 = jnp.bfloat16

RMS_EPS = 1e-6
TOP_K = 2
V7X_LANES = 128
V7X_SUBLANES = 8
V7X_VMEM_LIMIT_BYTES = 60 * 1024 * 1024

MIX_TIME_STEPS = 64
SCAN_COLS = 1024
FFN_ROWS = 1024
FFN_COLS = 512
MOE_ROWS = 512
MOE_COLS = 1792
COMBINE_TOKENS = 512


def _const_spec(shape):
    zeros = (0,) * len(shape)
    return pl.BlockSpec(shape, lambda *_: zeros, pipeline_mode=pl.Buffered(1))


def _rms(xf, g):
    return xf * lax.rsqrt(jnp.mean(xf * xf, axis=-1, keepdims=True) + RMS_EPS) * g


def _dot(a, b):
    return jnp.dot(a, b, preferred_element_type=F32)


def _mixer_kernel(*refs, d_conv, d_ssm, n_state_cols, n_experts, with_router):
    (x_ref, gmix_ref, win_ref, bin_ref, convw_ref, wconv_ref, bmat_ref, are_ref, aim_ref,
     cmat_ref, dskip_ref, wglu_ref, wo_ref, gffn_ref) = refs[:14]
    refs = refs[14:]
    if with_router:
        rw_ref, rb_ref = refs[:2]
        xo_ref, h2_ref, route_ref, counts_ref, cstart_ref, inv_ref = refs[2:8]
        ucv_ref, state_ref, s_ref, cnt_ref = refs[8:]
    else:
        xo_ref, h2_ref = refs[:2]
        ucv_ref, state_ref, s_ref = refs[2:]

    rows, d_model = x_ref.shape
    sub = V7X_SUBLANES
    half_cols = n_state_cols // 2
    step = pl.program_id(0)

    @pl.when(step == 0)
    def _():
        ucv_ref[0:2 * sub, :] = jnp.zeros((2 * sub, d_conv), F32)
        state_ref[...] = jnp.zeros_like(state_ref)
        if with_router:
            cnt_ref[...] = jnp.zeros_like(cnt_ref)

    x = x_ref[...]
    h = _rms(x, gmix_ref[...]).astype(BF16)

    def proj(lo, width):
        return _dot(h, win_ref[:, lo:lo + width]) + bin_ref[:, lo:lo + width]

    o_c, o_v, o_u = d_conv, 2 * d_conv, 3 * d_conv
    o_g = o_u + d_ssm

    ucv_ref[2 * sub:, :] = proj(o_c, d_conv) * proj(o_v, d_conv)
    conv = (convw_ref[0:1, :] * ucv_ref[0:rows, :]
            + convw_ref[1:2, :] * ucv_ref[sub:rows + sub, :]
            + convw_ref[2:3, :] * ucv_ref[2 * sub:, :])
    ucv_ref[0:2 * sub, :] = ucv_ref[rows:rows + 2 * sub, :]
    y_conv = _dot((proj(0, d_conv) * conv).astype(BF16), wconv_ref[...])

    u = proj(o_u, d_ssm)
    ub = u.astype(BF16)
    n_half = n_state_cols // half_cols
    u_cols = d_ssm // n_half
    for hh in range(n_half):
        s_ref[:, hh * half_cols:(hh + 1) * half_cols] = _dot(
            ub[:, hh * u_cols:(hh + 1) * u_cols], bmat_ref[hh])

    part = half_cols // 2
    for hh in range(n_half):
        for c0 in range(0, part, SCAN_COLS):
            re0 = hh * half_cols + c0
            im0 = re0 + part
            a_re = are_ref[:, hh * part + c0:hh * part + c0 + SCAN_COLS]
            a_im = aim_ref[:, hh * part + c0:hh * part + c0 + SCAN_COLS]

            def scan_step(t, carry, re0=re0, im0=im0, a_re=a_re, a_im=a_im):
                s_re, s_im = carry
                r0 = pl.multiple_of(t * sub, sub)
                b_re = s_ref[pl.ds(r0, sub), re0:re0 + SCAN_COLS]
                b_im = s_ref[pl.ds(r0, sub), im0:im0 + SCAN_COLS]
                n_re = a_re * s_re - a_im * s_im + b_re
                n_im = a_re * s_im + a_im * s_re + b_im
                s_ref[pl.ds(r0, sub), re0:re0 + SCAN_COLS] = n_re
                s_ref[pl.ds(r0, sub), im0:im0 + SCAN_COLS] = n_im
                return n_re, n_im

            init = (state_ref[:, re0:re0 + SCAN_COLS], state_ref[:, im0:im0 + SCAN_COLS])
            s_re, s_im = lax.fori_loop(0, rows // sub, scan_step, init)
            state_ref[:, re0:re0 + SCAN_COLS] = s_re
            state_ref[:, im0:im0 + SCAN_COLS] = s_im

    y = jnp.concatenate(
        [_dot(s_ref[:, hh * half_cols:(hh + 1) * half_cols].astype(BF16), cmat_ref[hh])
         for hh in range(n_half)], axis=1)
    y = jax.nn.gelu(y + dskip_ref[...] * u, approximate=True).astype(BF16)
    z = _dot(y, wglu_ref[...])
    y_ssm = z[:, :d_model] * jax.nn.sigmoid(z[:, d_model:])

    mix = (jax.nn.sigmoid(proj(o_g, d_model)) * y_conv
           + jax.nn.sigmoid(proj(o_g + d_model, d_model)) * y_ssm)
    xn = x + _dot(mix.astype(BF16), wo_ref[...])
    xo_ref[...] = xn
    h2 = _rms(xn, gffn_ref[...])

    if not with_router:
        h2_ref[...] = h2.astype(BF16)
        return

    h2_ref[...] = h2

    lanes = V7X_LANES
    lane = lax.broadcasted_iota(jnp.int32, (rows, lanes), 1).astype(F32)
    logits = _dot(h2.astype(BF16), rw_ref[...]) + rb_ref[...]
    lg = jnp.where(lane < n_experts, logits, -jnp.inf)
    m1 = jnp.max(lg, axis=1, keepdims=True)
    i1 = jnp.min(jnp.where(lg == m1, lane, float(lanes)), axis=1, keepdims=True)
    lg2 = jnp.where(lane == i1, -jnp.inf, lg)
    m2 = jnp.max(lg2, axis=1, keepdims=True)
    i2 = jnp.min(jnp.where(lg2 == m2, lane, float(lanes)), axis=1, keepdims=True)
    e2 = jnp.exp(m2 - m1)
    w1 = 1.0 / (1.0 + e2)
    w2 = e2 / (1.0 + e2)
    route = jnp.where(lane == 0, i1, 0.0)
    route = jnp.where(lane == 1, i2, route)
    route = jnp.where(lane == 2, w1, route)
    route = jnp.where(lane == 3, w2, route)
    route_ref[...] = route

    onehot = jnp.where((lane == i1) | (lane == i2), 1.0, 0.0)
    cstart_ref[...] = cnt_ref[...]
    cnt_ref[...] = cnt_ref[...] + jnp.sum(onehot, axis=0, keepdims=True)
    counts_ref[...] = cnt_ref[...]
    sub = V7X_SUBLANES
    n_bits = rows.bit_length() - 1
    routed = onehot.T[0:sub, :] > 0.5
    tok = lax.broadcasted_iota(jnp.int32, (sub, rows), 1)
    ones = jnp.where(routed, 1, 0)
    rank = ones
    for bit in range(n_bits):
        rank = rank + jnp.where(tok >= (1 << bit), pltpu.roll(rank, 1 << bit, 1), 0)
    dist = jnp.where(routed, tok - (rank - ones), 0)
    val = jnp.where(routed, tok, -1)
    for bit in range(n_bits):
        moving = (val >= 0) & (((dist >> bit) & 1) == 1)
        val_in = pltpu.roll(jnp.where(moving, val, -1), rows - (1 << bit), 1)
        dist_in = pltpu.roll(jnp.where(moving, dist, 0), rows - (1 << bit), 1)
        val = jnp.where(val_in >= 0, val_in, jnp.where(moving, -1, val))
        dist = jnp.where(val_in >= 0, dist_in, jnp.where(moving, 0, dist))
    inv_ref[...] = val


def _mixer(x_rows, p, router):
    t_rows, d_model = x_rows.shape
    rows = MIX_TIME_STEPS * V7X_SUBLANES
    assert t_rows % rows == 0
    d_conv = p["w_conv_out"].shape[0]
    d_ssm = p["w_glu"].shape[0]
    n_state_cols = p["a_re"].shape[1] * 2
    with_router = router is not None
    n_experts = router["n_experts"] if with_router else 0

    row_spec = lambda width: pl.BlockSpec((rows, width), lambda i: (i, 0))
    consts = [p["norm_mix"], p["w_in"], p["b_in"], p["conv_w"], p["w_conv_out"], p["bmat"],
              p["a_re"], p["a_im"], p["cmat"], p["ssm_d"], p["w_glu"], p["w_o"], p["norm_ffn"]]
    out_shape = [jax.ShapeDtypeStruct((t_rows, d_model), F32)]
    out_specs = [row_spec(d_model)]
    scratch = [pltpu.VMEM((rows + 2 * V7X_SUBLANES, d_conv), F32),
               pltpu.VMEM((V7X_SUBLANES, n_state_cols), F32),
               pltpu.VMEM((rows, n_state_cols), F32)]
    if with_router:
        consts += [router["w"], router["b"]]
        assert n_experts <= V7X_SUBLANES and rows & (rows - 1) == 0
        n_blocks = t_rows // rows
        sub_spec = lambda width: pl.BlockSpec((V7X_SUBLANES, width), lambda i: (i, 0))
        out_shape += [jax.ShapeDtypeStruct((t_rows, d_model), F32),
                      jax.ShapeDtypeStruct((t_rows, V7X_LANES), F32),
                      jax.ShapeDtypeStruct((V7X_SUBLANES, V7X_LANES), F32),
                      jax.ShapeDtypeStruct((n_blocks * V7X_SUBLANES, V7X_LANES), F32),
                      jax.ShapeDtypeStruct((n_blocks * V7X_SUBLANES, rows), jnp.int32)]
        out_specs += [row_spec(d_model), row_spec(V7X_LANES),
                      pl.BlockSpec((V7X_SUBLANES, V7X_LANES), lambda i: (0, 0)),
                      sub_spec(V7X_LANES), sub_spec(rows)]
        scratch += [pltpu.VMEM((V7X_SUBLANES, V7X_LANES), F32)]
    else:
        out_shape += [jax.ShapeDtypeStruct((t_rows, d_model), BF16)]
        out_specs += [row_spec(d_model)]

    kern = functools.partial(_mixer_kernel, d_conv=d_conv, d_ssm=d_ssm,
                             n_state_cols=n_state_cols, n_experts=n_experts,
                             with_router=with_router)
    return pl.pallas_call(
        kern,
        grid=(t_rows // rows,),
        in_specs=[row_spec(d_model)] + [_const_spec(c.shape) for c in consts],
        out_specs=out_specs,
        out_shape=out_shape,
        scratch_shapes=scratch,
        compiler_params=pltpu.CompilerParams(
            dimension_semantics=("arbitrary",), vmem_limit_bytes=V7X_VMEM_LIMIT_BYTES),
        name="mixer_router" if with_router else "mixer",
    )(x_rows, *consts)


def _swiglu_partial(hb, wg_ref, wu_ref, wd_ref):
    g = _dot(hb, wg_ref[...])
    a = (jax.nn.silu(g) * _dot(hb, wu_ref[...])).astype(BF16)
    return _dot(a, wd_ref[...])


def _dense_ffn_kernel(x_ref, h_ref, wg_ref, wu_ref, wd_ref, gfin_ref, o_ref, acc_ref, *,
                      final_norm):
    f = pl.program_id(1)

    @pl.when(f == 0)
    def _():
        acc_ref[...] = jnp.zeros_like(acc_ref)

    acc_ref[...] += _swiglu_partial(h_ref[...], wg_ref, wu_ref, wd_ref)

    @pl.when(f == pl.num_programs(1) - 1)
    def _():
        out = x_ref[...] + acc_ref[...]
        o_ref[...] = _rms(out, gfin_ref[...]) if final_norm else out


def _dense_ffn(x_rows, h2, w_gate, w_up, w_down, g_final, final_norm):
    t_rows, d_model = x_rows.shape
    d_ff = w_gate.shape[1]
    tm, tf = FFN_ROWS, FFN_COLS
    assert t_rows % tm == 0 and d_ff % tf == 0
    return pl.pallas_call(
        functools.partial(_dense_ffn_kernel, final_norm=final_norm),
        grid=(t_rows // tm, d_ff // tf),
        in_specs=[pl.BlockSpec((tm, d_model), lambda i, f: (i, 0)),
                  pl.BlockSpec((tm, d_model), lambda i, f: (i, 0)),
                  pl.BlockSpec((d_model, tf), lambda i, f: (0, f)),
                  pl.BlockSpec((d_model, tf), lambda i, f: (0, f)),
                  pl.BlockSpec((tf, d_model), lambda i, f: (f, 0)),
                  pl.BlockSpec((1, d_model), lambda i, f: (0, 0))],
        out_specs=pl.BlockSpec((tm, d_model), lambda i, f: (i, 0)),
        out_shape=jax.ShapeDtypeStruct((t_rows, d_model), F32),
        scratch_shapes=[pltpu.VMEM((tm, d_model), F32)],
        compiler_params=pltpu.CompilerParams(
            dimension_semantics=("arbitrary", "arbitrary"),
            vmem_limit_bytes=V7X_VMEM_LIMIT_BYTES),
        name="dense_swiglu",
    )(x_rows, h2, w_gate, w_up, w_down, g_final)


def _expert_kernel(tile_e_ref, nvalid_ref, src_ref, dst_ref, h_hbm, wg_ref, wu_ref, wd_ref,
                   y_hbm, hbuf_ref, ybuf_ref, hb_ref, acc_ref, gsem, ssem):
    del tile_e_ref
    s = pl.program_id(0)
    f = pl.program_id(1)
    n_f = pl.num_programs(1)
    n_tiles = pl.num_programs(0) - 3
    tm = hb_ref.shape[0]
    part = tm // n_f
    nv = nvalid_ref[0]
    c = s - 1
    gather_on = s < nv
    compute_on = (c >= 0) & (c < nv)
    scatter_on = (s >= 2) & (s - 2 < n_tiles)
    slot_s = lax.rem(s, 2)
    slot_c = 1 - slot_s

    base = pl.multiple_of(f * part, part)

    def issue_gather():
        for j in range(part):
            tok = src_ref[0, 0, base + j]
            pltpu.make_async_copy(h_hbm.at[pl.ds(tok, 1)],
                                  hbuf_ref.at[slot_s, pl.ds(base + j, 1)],
                                  gsem.at[slot_s]).start()

    def issue_scatter():
        for j in range(part):
            row = dst_ref[0, 0, base + j]
            pltpu.make_async_copy(ybuf_ref.at[slot_s, pl.ds(base + j, 1)],
                                  y_hbm.at[pl.ds(row, 1)], ssem.at[slot_s]).start()

    def wait_gather(slot):
        pltpu.make_async_copy(h_hbm.at[pl.ds(0, tm)], hbuf_ref.at[slot], gsem.at[slot]).wait()

    def wait_scatter(slot):
        pltpu.make_async_copy(ybuf_ref.at[slot], y_hbm.at[pl.ds(0, tm)], ssem.at[slot]).wait()

    @pl.when(compute_on & (f == 0))
    def _():
        wait_gather(slot_c)
        hb_ref[...] = hbuf_ref[slot_c].astype(BF16)
        acc_ref[...] = jnp.zeros_like(acc_ref)

    def compute():
        acc_ref[...] += _swiglu_partial(hb_ref[...], wg_ref, wu_ref, wd_ref)

    steady = gather_on & compute_on & scatter_on

    @pl.when(steady)
    def _():
        issue_gather()
        issue_scatter()
        compute()

    @pl.when(jnp.logical_not(steady))
    def _():
        pl.when(gather_on)(issue_gather)
        pl.when(scatter_on)(issue_scatter)
        pl.when(compute_on)(compute)

    @pl.when((f == n_f - 1) & (c >= 0) & (c < n_tiles + 2))
    def _():
        pl.when(c >= 2)(lambda: wait_scatter(slot_c))

        @pl.when(compute_on)
        def _():
            ybuf_ref[slot_c] = acc_ref[...]

        @pl.when(jnp.logical_not(compute_on) & (c < n_tiles))
        def _():
            ybuf_ref[slot_c] = jnp.zeros((tm, ybuf_ref.shape[2]), F32)


def _experts(h_rows, src_tok, dst_row, tile_e, nvalid, w_gate, w_up, w_down):
    d_model = h_rows.shape[1]
    d_ff = w_gate.shape[2]
    tm, tf = MOE_ROWS, MOE_COLS
    n_f = d_ff // tf
    n_tiles = tile_e.shape[0]
    p_rows = n_tiles * tm
    assert d_ff % tf == 0 and tm % n_f == 0

    def w_tile(s, nv):
        return jnp.clip(s - 1, 0, nv[0] - 1)

    def f_eff(s, f, nv):
        return jnp.where((s >= 1) & (s - 1 < nv[0]), f, jnp.where(s < 1, 0, n_f - 1))

    idx_spec = lambda shift: pl.BlockSpec(
        (1, 1, tm), lambda s, f, te, nv: (jnp.clip(s - shift, 0, n_tiles - 1), 0, 0),
        memory_space=pltpu.SMEM)
    grid_spec = pltpu.PrefetchScalarGridSpec(
        num_scalar_prefetch=2,
        grid=(n_tiles + 3, n_f),
        in_specs=[idx_spec(0), idx_spec(2),
                  pl.BlockSpec(memory_space=pl.ANY),
                  pl.BlockSpec((None, d_model, tf),
                               lambda s, f, te, nv: (te[w_tile(s, nv)], 0, f_eff(s, f, nv))),
                  pl.BlockSpec((None, d_model, tf),
                               lambda s, f, te, nv: (te[w_tile(s, nv)], 0, f_eff(s, f, nv))),
                  pl.BlockSpec((None, tf, d_model),
                               lambda s, f, te, nv: (te[w_tile(s, nv)], f_eff(s, f, nv), 0))],
        out_specs=pl.BlockSpec(memory_space=pl.ANY),
        scratch_shapes=[pltpu.VMEM((2, tm, d_model), F32), pltpu.VMEM((2, tm, d_model), F32),
                        pltpu.VMEM((tm, d_model), BF16), pltpu.VMEM((tm, d_model), F32),
                        pltpu.SemaphoreType.DMA((2,)), pltpu.SemaphoreType.DMA((2,))],
    )
    return pl.pallas_call(
        _expert_kernel,
        grid_spec=grid_spec,
        out_shape=jax.ShapeDtypeStruct((p_rows, d_model), F32),
        compiler_params=pltpu.CompilerParams(
            dimension_semantics=("arbitrary", "arbitrary"),
            vmem_limit_bytes=V7X_VMEM_LIMIT_BYTES, has_side_effects=True),
        name="moe_experts",
    )(tile_e, nvalid, src_tok.reshape(n_tiles, 1, tm), dst_row.reshape(n_tiles, 1, tm),
      h_rows, w_gate, w_up, w_down)


def _combine_kernel(x_ref, route_ref, y0_ref, y1_ref, gfin_ref, o_ref, *, final_norm):
    route = route_ref[...]
    out = x_ref[...] + (route[:, 2:3] * y0_ref[...] + route[:, 3:4] * y1_ref[...])
    o_ref[...] = _rms(out, gfin_ref[...]) if final_norm else out


def _combine(x_rows, route, y, g_final, final_norm):
    t_rows, d_model = x_rows.shape
    tb = COMBINE_TOKENS
    assert t_rows % tb == 0
    n_blk = t_rows // tb
    return pl.pallas_call(
        functools.partial(_combine_kernel, final_norm=final_norm),
        grid=(n_blk,),
        in_specs=[pl.BlockSpec((tb, d_model), lambda i: (i, 0)),
                  pl.BlockSpec((tb, V7X_LANES), lambda i: (i, 0)),
                  pl.BlockSpec((tb, d_model), lambda i: (i, 0)),
                  pl.BlockSpec((tb, d_model), lambda i: (i + n_blk, 0)),
                  pl.BlockSpec((1, d_model), lambda i: (0, 0))],
        out_specs=pl.BlockSpec((tb, d_model), lambda i: (i, 0)),
        out_shape=jax.ShapeDtypeStruct((t_rows, d_model), F32),
        compiler_params=pltpu.CompilerParams(dimension_semantics=("arbitrary",)),
        name="moe_combine",
    )(x_rows, route, y, y, g_final)


def _moe(x_rows, h_rows, routing, w_gate, w_up, w_down, g_final, final_norm):
    route, counts, cstart, inv = routing
    t_rows = x_rows.shape[0]
    n_experts = w_gate.shape[0]
    tm = MOE_ROWS
    n_tiles = TOP_K * t_rows // tm + n_experts
    p_rows = n_tiles * tm
    blk_rows = inv.shape[1]
    n_blocks = t_rows // blk_rows
    sub = V7X_SUBLANES

    idx = route[:, 0:TOP_K].astype(jnp.int32)
    cnt = counts[0, :n_experts].astype(jnp.int32)
    tiles = (cnt + tm - 1) // tm
    tile_end = jnp.cumsum(tiles)
    offset = (tile_end - tiles) * tm
    nvalid = tile_end[-1:]
    tile_id = jnp.minimum(jnp.arange(n_tiles, dtype=jnp.int32), nvalid[0] - 1)
    tile_e = jnp.sum((tile_id[:, None] >= tile_end[None, :]).astype(jnp.int32), axis=1)
    tile_e = jnp.minimum(tile_e, n_experts - 1)

    row = jnp.arange(p_rows, dtype=jnp.int32)
    e_row = tile_e[row // tm]
    j_row = row - offset[e_row]
    valid = (row // tm < nvalid[0]) & (j_row < cnt[e_row])
    blk_start = cstart.reshape(n_blocks, sub, -1)[:, 0, :n_experts].astype(jnp.int32)
    start_e = blk_start.T[e_row]
    blk = jnp.sum((start_e <= j_row[:, None]).astype(jnp.int32), axis=1) - 1
    m_row = j_row - jnp.take_along_axis(start_e, blk[:, None], axis=1)[:, 0]
    local = inv.reshape(n_blocks, sub, blk_rows)[blk, e_row, jnp.clip(m_row, 0, blk_rows - 1)]
    src_tok = jnp.where(valid, blk * blk_rows + local, 0)
    slot = jnp.where(idx[src_tok, 0] == e_row, 0, 1)
    spare = jnp.cumsum(jnp.logical_not(valid).astype(jnp.int32)) - 1
    dst_row = jnp.where(valid, slot * t_rows + src_tok, TOP_K * t_rows + spare)

    y = _experts(h_rows, src_tok, dst_row, tile_e, nvalid.astype(jnp.int32),
                 w_gate, w_up, w_down)
    return _combine(x_rows, route, y, g_final, final_norm)


def _ssm_matrices(a_re, a_im, log_dt, b_re, b_im, c_re, c_im):
    n_groups, n_state = a_re.shape
    grp = b_re.shape[2]
    dt = jnp.exp(log_dt)[:, None]
    mag = jnp.exp(a_re * dt)
    abar_re = mag * jnp.cos(a_im * dt)
    abar_im = mag * jnp.sin(a_im * dt)
    den = a_re * a_re + a_im * a_im
    nr = abar_re - 1.0
    ni = abar_im
    coef_re = (nr * a_re + ni * a_im) / den
    coef_im = (ni * a_re - nr * a_im) / den
    bb_re = coef_re[..., None] * b_re - coef_im[..., None] * b_im
    bb_im = coef_re[..., None] * b_im + coef_im[..., None] * b_re

    n_half = 2
    gh = n_groups // n_half
    eye = jnp.eye(gh, dtype=F32)

    def b_block(bb):
        return jnp.einsum("gnc,gh->gchn", bb, eye).reshape(gh * grp, gh * n_state)

    def c_block(cc):
        return jnp.einsum("gcn,gh->gnhc", cc, eye).reshape(gh * n_state, gh * grp)

    bmat, cmat, are, aim = [], [], [], []
    for hh in range(n_half):
        sl = slice(hh * gh, (hh + 1) * gh)
        bmat.append(jnp.concatenate([b_block(bb_re[sl]), b_block(bb_im[sl])], axis=1))
        cmat.append(jnp.concatenate([c_block(c_re[sl]), -c_block(c_im[sl])], axis=0))
        are.append(abar_re[sl].reshape(-1))
        aim.append(abar_im[sl].reshape(-1))
    bcast = lambda v: jnp.broadcast_to(jnp.concatenate(v)[None, :],
                                       (V7X_SUBLANES, n_groups * n_state))
    return (jnp.stack(bmat).astype(BF16), jnp.stack(cmat).astype(BF16), bcast(are), bcast(aim))


def kernel(x, norm_mix, w_in, b_in, conv_w, w_conv_out, ssm_a_re, ssm_a_im, ssm_log_dt,
           ssm_b_re, ssm_b_im, ssm_c_re, ssm_c_im, ssm_d, w_glu, w_o, norm_ffn,
           dense_w_gate, dense_w_up, dense_w_down, router_w, router_b,
           moe_w_gate, moe_w_up, moe_w_down, norm_final):
    bsz, seq, d_model = x.shape
    assert bsz == V7X_SUBLANES, "one time step must be one 8-row sublane tile"
    depth = w_in.shape[0]
    n_experts = router_w.shape[-1]
    t_rows = bsz * seq
    xr = jnp.transpose(x, (1, 0, 2)).reshape(t_rows, d_model)
    g_final = norm_final.reshape(1, d_model)

    for layer in range(depth):
        bmat, cmat, are, aim = _ssm_matrices(
            ssm_a_re[layer], ssm_a_im[layer], ssm_log_dt[layer], ssm_b_re[layer],
            ssm_b_im[layer], ssm_c_re[layer], ssm_c_im[layer])
        p = dict(norm_mix=norm_mix[layer].reshape(1, -1), w_in=w_in[layer].astype(BF16),
                 b_in=b_in[layer].reshape(1, -1), conv_w=conv_w[layer],
                 w_conv_out=w_conv_out[layer].astype(BF16), bmat=bmat, cmat=cmat,
                 a_re=are, a_im=aim, ssm_d=ssm_d[layer].reshape(1, -1),
                 w_glu=w_glu[layer].astype(BF16), w_o=w_o[layer].astype(BF16),
                 norm_ffn=norm_ffn[layer].reshape(1, -1))
        last = layer == depth - 1
        i = layer // 2
        if layer % 2 == 0:
            xr, h2 = _mixer(xr, p, None)
            xr = _dense_ffn(xr, h2, dense_w_gate[i].astype(BF16), dense_w_up[i].astype(BF16),
                            dense_w_down[i].astype(BF16), g_final, last)
        else:
            rw = jnp.zeros((d_model, V7X_LANES), F32).at[:, :n_experts].set(router_w[i])
            rb = jnp.zeros((1, V7X_LANES), F32).at[0, :n_experts].set(router_b[i])
            router = dict(w=rw.astype(BF16), b=rb, n_experts=n_experts)
            xr, h2, *routing = _mixer(xr, p, router)
            xr = _moe(xr, h2, routing, moe_w_gate[i].astype(BF16),
                      moe_w_up[i].astype(BF16), moe_w_down[i].astype(BF16), g_final, last)
    return jnp.transpose(xr.reshape(seq, bsz, d_model), (1, 0, 2))
```
